```python
import jax, jax.numpy as jnp
from jax import lax
import numpy as np

D_MODEL = 1024
BATCH = 16
SEQ = 2048
DEPTH = 1

HEAD_DIM = 64
N_Q_HEADS = 16
N_KV_HEADS = 2
GQA_GROUP = N_Q_HEADS // N_KV_HEADS
WINDOW = 128
BLOCK = 128
Q_WIDTH = N_Q_HEADS * HEAD_DIM
KV_WIDTH = N_KV_HEADS * HEAD_DIM
CONV_CH = D_MODEL
CONV_WIDTH = 31
N_BRANCH = 2
IN_WIDTH = Q_WIDTH + 2 * KV_WIDTH + 2 * CONV_CH + N_BRANCH * D_MODEL
D_FF = 2816
FFN_RESIDUAL = 0.5
N_MOD = 9
EPS = 1e-6

kernel_name = "conditioned_hybrid_swa_conformer_macaron_layer"


def rmsnorm(x, g):
    xf = x.astype(jnp.float32)
    y = xf * lax.rsqrt(jnp.mean(xf * xf, axis=-1, keepdims=True) + EPS)
    return (y * g.astype(jnp.float32)).astype(x.dtype)


def layernorm(x, g, b):
    xf = x.astype(jnp.float32)
    mu = jnp.mean(xf, axis=-1, keepdims=True)
    var = jnp.mean(jnp.square(xf - mu), axis=-1, keepdims=True)
    y = (xf - mu) * lax.rsqrt(var + EPS)
    return (y * g.astype(jnp.float32) + b.astype(jnp.float32)).astype(x.dtype)


def modulate(h, shift, scale):
    return h * (1 + scale[:, None, :]) + shift[:, None, :]


def swiglu(h, w_gate, w_up, w_down):
    return (jax.nn.silu(h @ w_gate) * (h @ w_up)) @ w_down


def sliding_window_sink_attention(q, k, v, sinks):
    B, S = q.shape[0], q.shape[1]
    nb = S // BLOCK
    qb = q.reshape(B, nb, BLOCK, N_KV_HEADS, GQA_GROUP, HEAD_DIM)

    def band(t):
        tp = jnp.pad(t, ((0, 0), (BLOCK, 0), (0, 0), (0, 0)))
        tb = tp.reshape(B, nb + 1, BLOCK, N_KV_HEADS, HEAD_DIM)
        return jnp.concatenate([tb[:, :-1], tb[:, 1:]], axis=2)

    kb, vb = band(k), band(v)
    scores = jnp.einsum('bnqkgd,bnskd->bnkgqs', qb, kb).astype(jnp.float32) * (HEAD_DIM ** -0.5)
    qi = jnp.arange(BLOCK)[:, None]
    sj = jnp.arange(2 * BLOCK)[None, :]
    rel = qi + BLOCK - sj
    key_pos = jnp.arange(nb)[:, None, None] * BLOCK + sj[None] - BLOCK
    valid = ((rel >= 0) & (rel < WINDOW))[None] & (key_pos >= 0)
    valid = valid[None, :, None, None]
    sink = sinks.astype(jnp.float32).reshape(1, 1, N_KV_HEADS, GQA_GROUP, 1, 1)
    masked = jnp.where(valid, scores, -jnp.inf)
    m = jnp.maximum(jnp.max(masked, axis=-1, keepdims=True), sink)
    p = jnp.where(valid, jnp.exp(masked - m), 0.0)
    denom = jnp.sum(p, axis=-1, keepdims=True) + jnp.exp(sink - m)
    probs = (p / denom).astype(v.dtype)
    out = jnp.einsum('bnkgqs,bnskd->bnqkgd', probs, vb)
    return out.reshape(B, S, Q_WIDTH)


def conformer_conv(u2, w_dw, b_dw, ln_g, ln_b, w_pw):
    a, b = jnp.split(u2, 2, axis=-1)
    u = a * jax.nn.sigmoid(b)
    u = jnp.pad(u, ((0, 0), (CONV_WIDTH - 1, 0), (0, 0)))
    y = lax.conv_general_dilated(
        u, w_dw[:, None, :].astype(u.dtype), window_strides=(1,), padding='VALID',
        dimension_numbers=('NWC', 'WIO', 'NWC'), feature_group_count=CONV_CH)
    y = y + b_dw
    y = jax.nn.silu(layernorm(y, ln_g, ln_b))
    return y @ w_pw


def setup_inputs(seed: int = 0) -> dict:
    key = jax.random.key(seed)
    ks = jax.random.split(key, 24)
    f32 = jnp.float32
    L = DEPTH

    def w(k, shape, fan_in):
        return jax.random.normal(k, shape, f32) * (fan_in ** -0.5)

    def gain(k, shape):
        return 1.0 + 0.02 * jax.random.normal(k, shape, f32)

    def small(k, shape):
        return 0.01 * jax.random.normal(k, shape, f32)

    return {
        "x": jax.random.normal(ks[0], (BATCH, SEQ, D_MODEL), f32),
        "c": jax.random.normal(ks[1], (BATCH, D_MODEL), f32),
        "w_ada": w(ks[2], (L, D_MODEL, N_MOD * D_MODEL), D_MODEL),
        "b_ada": small(ks[3], (L, N_MOD * D_MODEL)),
        "norm_ffn1_g": gain(ks[4], (L, D_MODEL)),
        "ffn1_w_gate": w(ks[5], (L, D_MODEL, D_FF), D_MODEL),
        "ffn1_w_up": w(ks[6], (L, D_MODEL, D_FF), D_MODEL),
        "ffn1_w_down": w(ks[7], (L, D_FF, D_MODEL), D_FF),
        "norm_mix_g": gain(ks[8], (L, D_MODEL)),
        "w_in": w(ks[9], (L, D_MODEL, IN_WIDTH), D_MODEL),
        "attn_sinks": 0.5 * jax.random.normal(ks[10], (L, N_Q_HEADS), f32),
        "w_attn_o": w(ks[11], (L, Q_WIDTH, D_MODEL), Q_WIDTH),
        "conv_w_dw": w(ks[12], (L, CONV_WIDTH, CONV_CH), CONV_WIDTH),
        "conv_b_dw": small(ks[13], (L, CONV_CH)),
        "conv_ln_g": gain(ks[14], (L, CONV_CH)),
        "conv_ln_b": small(ks[15], (L, CONV_CH)),
        "w_conv_o": w(ks[16], (L, CONV_CH, D_MODEL), CONV_CH),
        "w_out": w(ks[17], (L, D_MODEL, D_MODEL), D_MODEL),
        "norm_ffn2_g": gain(ks[18], (L, D_MODEL)),
        "ffn2_w_gate": w(ks[19], (L, D_MODEL, D_FF), D_MODEL),
        "ffn2_w_up": w(ks[20], (L, D_MODEL, D_FF), D_MODEL),
        "ffn2_w_down": w(ks[21], (L, D_FF, D_MODEL), D_FF),
        "final_norm_g": gain(ks[22], (D_MODEL,)),
    }


def reference(x, c, w_ada, b_ada, norm_ffn1_g, ffn1_w_gate, ffn1_w_up, ffn1_w_down,
              norm_mix_g, w_in, attn_sinks, w_attn_o, conv_w_dw, conv_b_dw, conv_ln_g,
              conv_ln_b, w_conv_o, w_out, norm_ffn2_g, ffn2_w_gate, ffn2_w_up, ffn2_w_down,
              final_norm_g):
    B, S, _ = x.shape
    c_act = jax.nn.silu(c)
    split_idx = np.cumsum([Q_WIDTH, KV_WIDTH, KV_WIDTH, 2 * CONV_CH, D_MODEL]).tolist()
    for l in range(DEPTH):
        mod = (c_act @ w_ada[l] + b_ada[l]).reshape(B, N_MOD, D_MODEL)
        sh1, sc1, g1 = mod[:, 0], mod[:, 1], mod[:, 2]
        sh2, sc2, g2 = mod[:, 3], mod[:, 4], mod[:, 5]
        sh3, sc3, g3 = mod[:, 6], mod[:, 7], mod[:, 8]

        h = modulate(rmsnorm(x, norm_ffn1_g[l]), sh1, sc1)
        x = x + FFN_RESIDUAL * g1[:, None, :] * swiglu(h, ffn1_w_gate[l], ffn1_w_up[l], ffn1_w_down[l])

        h = modulate(rmsnorm(x, norm_mix_g[l]), sh2, sc2)
        proj = h @ w_in[l]
        q, k, v, conv_in, gate_a, gate_c = jnp.split(proj, split_idx, axis=-1)
        q = q.reshape(B, S, N_Q_HEADS, HEAD_DIM)
        k = k.reshape(B, S, N_KV_HEADS, HEAD_DIM)
        v = v.reshape(B, S, N_KV_HEADS, HEAD_DIM)
        y_attn = sliding_window_sink_attention(q, k, v, attn_sinks[l]) @ w_attn_o[l]
        y_conv = conformer_conv(conv_in, conv_w_dw[l], conv_b_dw[l], conv_ln_g[l],
                                conv_ln_b[l], w_conv_o[l])
        merged = jax.nn.sigmoid(gate_a) * y_attn + jax.nn.sigmoid(gate_c) * y_conv
        x = x + g2[:, None, :] * (merged @ w_out[l])

        h = modulate(rmsnorm(x, norm_ffn2_g[l]), sh3, sc3)
        x = x + FFN_RESIDUAL * g3[:, None, :] * swiglu(h, ffn2_w_gate[l], ffn2_w_up[l], ffn2_w_down[l])
    return rmsnorm(x, final_norm_g)
```

```python
import functools

import jax
import jax.numpy as jnp
from jax import lax
from jax.experimental import pallas as pl
from jax.experimental.pallas import tpu as pltpu

F32 = jnp.float32
BF16 = jnp.bfloat16

EPS = 1e-6
HEAD_DIM = 64
N_Q_HEADS = 16
N_KV_HEADS = 2
GQA_GROUP = N_Q_HEADS // N_KV_HEADS
BLOCK = 128
CONV_WIDTH = 31
CONV_HALO = 32
SUBLANES = 8
NEG_BIG = -1e30

VMEM_LIMIT_BYTES = 56 * 1024 * 1024


def _resident(shape):
    nd = len(shape)
    return pl.BlockSpec(shape, lambda *_: (0,) * nd, pipeline_mode=pl.Buffered(1))


def _params(n_axes):
    return pltpu.CompilerParams(
        dimension_semantics=("arbitrary",) * n_axes,
        vmem_limit_bytes=VMEM_LIMIT_BYTES,
    )


def _sigmoid(x):
    return 1.0 / (1.0 + jnp.exp(-x))


def _norm_mod(x, gain, shift, scale):
    ms = jnp.mean(x * x, axis=-1, keepdims=True)
    y = x * lax.rsqrt(ms + EPS) * gain
    return (y * (1.0 + scale) + shift).astype(BF16)


def _ada_kernel(c_ref, w_ref, b_ref, o_ref):
    c = c_ref[...]
    ca = (c * _sigmoid(c)).astype(BF16)
    o_ref[...] = jnp.dot(ca, w_ref[...].astype(BF16), preferred_element_type=F32) + b_ref[...]


def _ada(c, w, b, tn=1024):
    bsz, d = c.shape
    n = w.shape[1]
    return pl.pallas_call(
        _ada_kernel,
        out_shape=jax.ShapeDtypeStruct((bsz, n), F32),
        grid=(n // tn,),
        in_specs=[
            pl.BlockSpec((bsz, d), lambda j: (0, 0)),
            pl.BlockSpec((d, tn), lambda j: (0, j)),
            pl.BlockSpec((1, tn), lambda j: (0, j)),
        ],
        out_specs=pl.BlockSpec((bsz, tn), lambda j: (0, j)),
        compiler_params=_params(1),
        name="ada",
    )(c, w, b.reshape(1, n))


def _ffn_kernel(x_ref, sh_ref, sc_ref, g_ref, ng_ref, wg_ref, wu_ref, wd_ref, *rest, final_norm):
    if final_norm:
        fg_ref, o_ref = rest
    else:
        (o_ref,) = rest
    x = x_ref[0]
    h = _norm_mod(x, ng_ref[...], sh_ref[0], sc_ref[0])
    gate = jnp.dot(h, wg_ref[...], preferred_element_type=F32)
    up = jnp.dot(h, wu_ref[...], preferred_element_type=F32)
    act = (gate * _sigmoid(gate) * up).astype(BF16)
    dn = jnp.dot(act, wd_ref[...], preferred_element_type=F32)
    xn = x + (0.5 * g_ref[0]) * dn
    if final_norm:
        ms = jnp.mean(xn * xn, axis=-1, keepdims=True)
        xn = xn * lax.rsqrt(ms + EPS) * fg_ref[...]
    o_ref[0] = xn


def _ffn(x, shift, scale, gate, norm_g, wg, wu, wd, final_g=None, tm=512):
    bsz, s, d = x.shape
    dff = wg.shape[1]
    row = pl.BlockSpec((1, tm, d), lambda b, i: (b, i, 0))
    per_batch = pl.BlockSpec((1, 1, d), lambda b, i: (b, 0, 0))
    in_specs = [row, per_batch, per_batch, per_batch, _resident((1, d)),
                _resident((d, dff)), _resident((d, dff)), _resident((dff, d))]
    args = [x, shift, scale, gate, norm_g.reshape(1, d), wg, wu, wd]
    if final_g is not None:
        in_specs.append(_resident((1, d)))
        args.append(final_g.reshape(1, d))
    return pl.pallas_call(
        functools.partial(_ffn_kernel, final_norm=final_g is not None),
        out_shape=jax.ShapeDtypeStruct(x.shape, F32),
        grid=(bsz, s // tm),
        in_specs=in_specs,
        out_specs=row,
        compiler_params=_params(2),
        name="ffn_final" if final_g is not None else "ffn",
    )(*args)


def _proj_kernel(x_ref, sh_ref, sc_ref, ng_ref, w_ref, q_ref, kv_ref, u_ref, ga_ref, gc_ref,
                 *, q_w, kv_w, c_w, d_w):
    h = _norm_mod(x_ref[0], ng_ref[...], sh_ref[0], sc_ref[0])

    def cols(lo, n):
        return jnp.dot(h, w_ref[:, lo:lo + n], preferred_element_type=F32)

    o = 0
    q_ref[0] = (cols(o, q_w) * (HEAD_DIM ** -0.5)).astype(BF16)
    o += q_w
    kv_ref[0] = cols(o, kv_w).astype(BF16)
    o += kv_w
    a = cols(o, c_w)
    o += c_w
    b = cols(o, c_w)
    o += c_w
    u_ref[0] = (a * _sigmoid(b)).astype(BF16)
    ga_ref[0] = _sigmoid(cols(o, d_w)).astype(BF16)
    o += d_w
    gc_ref[0] = _sigmoid(cols(o, d_w)).astype(BF16)


def _proj(x, shift, scale, norm_g, w_in, tm=512):
    bsz, s, d = x.shape
    q_w = N_Q_HEADS * HEAD_DIM
    kv_w = 2 * N_KV_HEADS * HEAD_DIM
    c_w = d
    row = lambda n: pl.BlockSpec((1, tm, n), lambda b, i: (b, i, 0))
    per_batch = pl.BlockSpec((1, 1, d), lambda b, i: (b, 0, 0))
    out = lambda n: jax.ShapeDtypeStruct((bsz, s, n), BF16)
    return pl.pallas_call(
        functools.partial(_proj_kernel, q_w=q_w, kv_w=kv_w, c_w=c_w, d_w=d),
        out_shape=[out(q_w), out(kv_w), out(c_w), out(d), out(d)],
        grid=(bsz, s // tm),
        in_specs=[row(d), per_batch, per_batch, _resident((1, d)), _resident(w_in.shape)],
        out_specs=[row(q_w), row(kv_w), row(c_w), row(d), row(d)],
        compiler_params=_params(2),
        name="proj",
    )(x, shift, scale, norm_g.reshape(1, d), w_in)


def _attn_kernel(sink_ref, q_ref, kvp_ref, kvc_ref, o_ref):
    n = pl.program_id(1)
    q = q_ref[0]
    kvp = kvp_ref[0]
    kvc = kvc_ref[0]
    k_w = N_KV_HEADS * HEAD_DIM
    qi = lax.broadcasted_iota(jnp.int32, (BLOCK, BLOCK), 0)
    cj = lax.broadcasted_iota(jnp.int32, (BLOCK, BLOCK), 1)
    own = cj <= qi
    prev_ok = jnp.logical_and(jnp.logical_not(own), n > 0)
    outs = []
    for j in range(N_KV_HEADS):
        ks = slice(j * HEAD_DIM, (j + 1) * HEAD_DIM)
        vs = slice(k_w + j * HEAD_DIM, k_w + (j + 1) * HEAD_DIM)
        kb = jnp.concatenate([kvp[:, ks], kvc[:, ks]], axis=0)
        vb = jnp.concatenate([kvp[:, vs], kvc[:, vs]], axis=0)
        for g in range(GQA_GROUP):
            hq = j * GQA_GROUP + g
            qh = q[:, hq * HEAD_DIM:(hq + 1) * HEAD_DIM]
            s2 = lax.dot_general(qh, kb, (((1,), (1,)), ((), ())), preferred_element_type=F32)
            sc = jnp.where(own, s2[:, BLOCK:], jnp.where(prev_ok, s2[:, :BLOCK], NEG_BIG))
            sink = sink_ref[hq]
            m = jnp.maximum(jnp.max(sc, axis=-1, keepdims=True), sink)
            p = jnp.exp(sc - m)
            denom = jnp.sum(p, axis=-1, keepdims=True) + jnp.exp(sink - m)
            pn = p / denom
            p2 = jnp.concatenate([jnp.where(own, 0.0, pn), jnp.where(own, pn, 0.0)], axis=1)
            outs.append(jnp.dot(p2.astype(BF16), vb, preferred_element_type=F32))
    o_ref[0] = jnp.concatenate(outs, axis=1).astype(BF16)


def _attn(q, kv, sinks):
    bsz, s, qw = q.shape
    kvw = kv.shape[2]
    nb = s // BLOCK
    return pl.pallas_call(
        _attn_kernel,
        out_shape=jax.ShapeDtypeStruct(q.shape, BF16),
        grid=(bsz, nb),
        in_specs=[
            pl.BlockSpec(memory_space=pltpu.SMEM),
            pl.BlockSpec((1, BLOCK, qw), lambda b, n: (b, n, 0)),
            pl.BlockSpec((1, BLOCK, kvw), lambda b, n: (b, jnp.maximum(n - 1, 0), 0)),
            pl.BlockSpec((1, BLOCK, kvw), lambda b, n: (b, n, 0)),
        ],
        out_specs=pl.BlockSpec((1, BLOCK, qw), lambda b, n: (b, n, 0)),
        compiler_params=_params(2),
        name="attn",
    )(sinks, q, kv, kv)


def _mix_kernel(x_ref, at_ref, u_ref, uh_ref, ga_ref, gc_ref, g2_ref, wdw_ref, bdw_ref,
                lng_ref, lnb_ref, wao_ref, wco_ref, wo_ref, o_ref, s_ref, y_ref, *, tm, rc, lc):
    i = pl.program_id(1)
    halo = uh_ref[0].astype(F32)
    s_ref[0:CONV_HALO, :] = jnp.where(i > 0, halo, 0.0)
    s_ref[CONV_HALO:, :] = u_ref[0].astype(F32)
    base = CONV_HALO - (CONV_WIDTH - 1)
    n_ch = s_ref.shape[1]

    def conv_lanes(r0, c0):
        win = s_ref[pl.ds(r0, rc + CONV_HALO), c0:c0 + lc]
        acc = jnp.zeros((rc, lc), F32)
        for s in range(SUBLANES):
            offs = [o for o in range(base, base + CONV_WIDTH) if o % SUBLANES == s]
            span = offs[-1] - s + rc
            shifted = win[s:s + span, :]
            for o in offs:
                k = o - base
                acc = acc + wdw_ref[k:k + 1, c0:c0 + lc] * shifted[o - s:o - s + rc, :]
        return acc

    def chunk(r, carry):
        r0 = pl.multiple_of(r * rc, rc)
        acc = jnp.concatenate([conv_lanes(r0, c0) for c0 in range(0, n_ch, lc)], axis=1)
        y = acc + bdw_ref[...]
        mu = jnp.mean(y, axis=-1, keepdims=True)
        yc = y - mu
        var = jnp.mean(yc * yc, axis=-1, keepdims=True)
        z = yc * lax.rsqrt(var + EPS) * lng_ref[...] + lnb_ref[...]
        y_ref[pl.ds(r0, rc), :] = (z * _sigmoid(z)).astype(BF16)
        return carry

    lax.fori_loop(0, tm // rc, chunk, 0)
    y_conv = jnp.dot(y_ref[...], wco_ref[...], preferred_element_type=F32)
    y_attn = jnp.dot(at_ref[0], wao_ref[...], preferred_element_type=F32)
    merged = ga_ref[0].astype(F32) * y_attn + gc_ref[0].astype(F32) * y_conv
    out = jnp.dot(merged.astype(BF16), wo_ref[...], preferred_element_type=F32)
    o_ref[0] = x_ref[0] + g2_ref[0] * out


def _mix(x, attn, u, sga, sgc, g2, w_dw, b_dw, ln_g, ln_b, w_ao, w_co, w_o, tm=512, rc=32,
         lc=256):
    bsz, s, d = x.shape
    c = u.shape[2]
    row = lambda n: pl.BlockSpec((1, tm, n), lambda b, i: (b, i, 0))
    per_batch = pl.BlockSpec((1, 1, d), lambda b, i: (b, 0, 0))
    hpt = tm // CONV_HALO
    halo = pl.BlockSpec((1, CONV_HALO, c), lambda b, i: (b, jnp.maximum(i * hpt - 1, 0), 0))
    wdw = jnp.concatenate([w_dw, jnp.zeros((CONV_HALO - CONV_WIDTH, c), F32)], axis=0)
    vec = lambda v: v.reshape(1, -1)
    return pl.pallas_call(
        functools.partial(_mix_kernel, tm=tm, rc=rc, lc=lc),
        out_shape=jax.ShapeDtypeStruct(x.shape, F32),
        grid=(bsz, s // tm),
        in_specs=[row(d), row(d), row(c), halo, row(d), row(d), per_batch,
                  _resident(wdw.shape), _resident((1, c)), _resident((1, c)), _resident((1, c)),
                  _resident(w_ao.shape), _resident(w_co.shape), _resident(w_o.shape)],
        out_specs=row(d),
        scratch_shapes=[pltpu.VMEM((tm + CONV_HALO, c), F32), pltpu.VMEM((tm, c), BF16)],
        compiler_params=_params(2),
        name="mix",
    )(x, attn, u, u, sga, sgc, g2, wdw, vec(b_dw), vec(ln_g), vec(ln_b), w_ao, w_co, w_o)


def kernel(x, c, w_ada, b_ada, norm_ffn1_g, ffn1_w_gate, ffn1_w_up, ffn1_w_down, norm_mix_g, w_in,
           attn_sinks, w_attn_o, conv_w_dw, conv_b_dw, conv_ln_g, conv_ln_b, w_conv_o, w_out,
           norm_ffn2_g, ffn2_w_gate, ffn2_w_up, ffn2_w_down, final_norm_g):
    bsz, s, d = x.shape
    depth = w_ada.shape[0]
    bf = lambda w: w.astype(BF16)
    for l in range(depth):
        mod = _ada(c, w_ada[l], b_ada[l]).reshape(bsz, -1, 1, d)
        sh1, sc1, g1, sh2, sc2, g2, sh3, sc3, g3 = (mod[:, k] for k in range(9))
        x = _ffn(x, sh1, sc1, g1, norm_ffn1_g[l], bf(ffn1_w_gate[l]), bf(ffn1_w_up[l]),
                 bf(ffn1_w_down[l]))
        q, kv, u, sga, sgc = _proj(x, sh2, sc2, norm_mix_g[l], bf(w_in[l]))
        attn = _attn(q, kv, attn_sinks[l])
        x = _mix(x, attn, u, sga, sgc, g2, conv_w_dw[l], conv_b_dw[l], conv_ln_g[l], conv_ln_b[l],
                 bf(w_attn_o[l]), bf(w_conv_o[l]), bf(w_out[l]))
        last = l == depth - 1
        x = _ffn(x, sh3, sc3, g3, norm_ffn2_g[l], bf(ffn2_w_gate[l]), bf(ffn2_w_up[l]),
                 bf(ffn2_w_down[l]), final_g=final_norm_g if last else None)
    return x
```

```python
import functools

import jax
import jax.numpy as jnp
from jax import lax
from jax.experimental import pallas as pl
from jax.experimental.pallas import tpu as pltpu

F32 = jnp.float32
BF16 = jnp.bfloat16

EPS = 1e-6
HEAD_DIM = 64
N_Q_HEADS = 16
N_KV_HEADS = 2
GQA_GROUP = N_Q_HEADS // N_KV_HEADS
BLOCK = 128
CONV_WIDTH = 31
CONV_HALO = 32
SUBLANES = 8
MXU_TILE = 256
NEG_BIG = -1e30

VMEM_LIMIT_BYTES = 56 * 1024 * 1024


def _resident(shape):
    nd = len(shape)
    return pl.BlockSpec(shape, lambda *_: (0,) * nd, pipeline_mode=pl.Buffered(1))


def _params(n_axes):
    return pltpu.CompilerParams(
        dimension_semantics=("arbitrary",) * n_axes,
        vmem_limit_bytes=VMEM_LIMIT_BYTES,
    )


def _sigmoid(x):
    return 1.0 / (1.0 + jnp.exp(-x))


def _norm_mod(x, gain, shift, scale):
    ms = jnp.mean(x * x, axis=-1, keepdims=True)
    y = x * lax.rsqrt(ms + EPS) * gain
    return (y * (1.0 + scale) + shift).astype(BF16)


def _ada_kernel(c_ref, w_ref, b_ref, o_ref):
    c = c_ref[...]
    ca = (c * _sigmoid(c)).astype(BF16)
    o_ref[...] = jnp.dot(ca, w_ref[...].astype(BF16), preferred_element_type=F32) + b_ref[...]


def _ada(c, w, b, tn=1024):
    bsz, d = c.shape
    n = w.shape[1]
    return pl.pallas_call(
        _ada_kernel,
        out_shape=jax.ShapeDtypeStruct((bsz, n), F32),
        grid=(n // tn,),
        in_specs=[
            pl.BlockSpec((bsz, d), lambda j: (0, 0)),
            pl.BlockSpec((d, tn), lambda j: (0, j)),
            pl.BlockSpec((1, tn), lambda j: (0, j)),
        ],
        out_specs=pl.BlockSpec((bsz, tn), lambda j: (0, j)),
        compiler_params=_params(1),
        name="ada",
    )(c, w, b.reshape(1, n))


def _ffn_kernel(x_ref, sh_ref, sc_ref, g_ref, ng_ref, wg_ref, wu_ref, wd_ref, *rest, final_norm):
    if final_norm:
        fg_ref, o_ref = rest
    else:
        (o_ref,) = rest
    x = x_ref[0]
    h = _norm_mod(x, ng_ref[...], sh_ref[0], sc_ref[0])
    gate = jnp.dot(h, wg_ref[...], preferred_element_type=F32)
    up = jnp.dot(h, wu_ref[...], preferred_element_type=F32)
    act = (gate * _sigmoid(gate) * up).astype(BF16)
    dn = jnp.dot(act, wd_ref[...], preferred_element_type=F32)
    xn = x + (0.5 * g_ref[0]) * dn
    if final_norm:
        ms = jnp.mean(xn * xn, axis=-1, keepdims=True)
        xn = xn * lax.rsqrt(ms + EPS) * fg_ref[...]
    o_ref[0] = xn


def _ffn(x, shift, scale, gate, norm_g, wg, wu, wd, final_g=None, tm=512):
    bsz, s, d = x.shape
    dff = wg.shape[1]
    row = pl.BlockSpec((1, tm, d), lambda b, i: (b, i, 0))
    per_batch = pl.BlockSpec((1, 1, d), lambda b, i: (b, 0, 0))
    in_specs = [row, per_batch, per_batch, per_batch, _resident((1, d)),
                _resident((d, dff)), _resident((d, dff)), _resident((dff, d))]
    args = [x, shift, scale, gate, norm_g.reshape(1, d), wg, wu, wd]
    if final_g is not None:
        in_specs.append(_resident((1, d)))
        args.append(final_g.reshape(1, d))
    return pl.pallas_call(
        functools.partial(_ffn_kernel, final_norm=final_g is not None),
        out_shape=jax.ShapeDtypeStruct(x.shape, F32),
        grid=(bsz, s // tm),
        in_specs=in_specs,
        out_specs=row,
        compiler_params=_params(2),
        name="ffn_final" if final_g is not None else "ffn",
    )(*args)


def _lane_half_mask(rows):
    return lax.broadcasted_iota(jnp.int32, (rows, 2 * HEAD_DIM), 1) < HEAD_DIM


def _conv_piece(s_ref, wdw_ref, r0, c0, rc, lc):
    base = CONV_HALO - (CONV_WIDTH - 1)
    wrows = rc + CONV_HALO
    win = s_ref[r0:r0 + wrows, c0:c0 + lc]
    acc = None
    for s in range(SUBLANES):
        sh = win if s == 0 else pltpu.roll(win, wrows - s, axis=0)
        for o in range(base + (s - base) % SUBLANES, base + CONV_WIDTH, SUBLANES):
            k = o - base
            term = wdw_ref[k:k + 1, c0:c0 + lc] * sh[o - s:o - s + rc, :]
            acc = term if acc is None else acc + term
    return acc


def _interleave(primary, filler):
    done = 0
    for idx, job in enumerate(primary):
        job()
        want = ((idx + 1) * len(filler)) // len(primary)
        while done < want:
            filler[done]()
            done += 1


def _mixer_kernel(sink_ref, x_ref, sh_ref, sc_ref, g2_ref, ng_ref, win_ref, wdw_ref, bdw_ref,
                  lng_ref, lnb_ref, wao_ref, wco_ref, wo_ref, o_ref,
                  s_ref, k_ref, v_ref, q_ref, at_ref, y_ref, *, tm, rc, lc):
    i = pl.program_id(1)
    d = x_ref.shape[2]
    q_w = N_Q_HEADS * HEAD_DIM
    pair_w = 2 * HEAD_DIM
    kv_w = N_KV_HEADS * HEAD_DIM
    nt = MXU_TILE

    @pl.when(i == 0)
    def _():
        s_ref[0:CONV_HALO, :] = jnp.zeros((CONV_HALO, s_ref.shape[1]), F32)
        k_ref[0:BLOCK, :] = jnp.zeros((BLOCK, k_ref.shape[1]), BF16)
        v_ref[0:BLOCK, :] = jnp.zeros((BLOCK, v_ref.shape[1]), BF16)

    x = x_ref[0]
    h = _norm_mod(x, ng_ref[...], sh_ref[0], sc_ref[0])

    def cols(lo, n):
        return jnp.dot(h, win_ref[:, lo:lo + n], preferred_element_type=F32)

    o_ab = q_w + 2 * kv_w
    s_ref[CONV_HALO:, :] = cols(o_ab, d) * _sigmoid(cols(o_ab + d, d))

    def kv_job():
        kvp = cols(q_w, 2 * kv_w)
        lo_t = _lane_half_mask(tm)
        for src, dst in ((kvp[:, :kv_w], k_ref), (kvp[:, kv_w:], v_ref)):
            rolled = pltpu.roll(src, HEAD_DIM, axis=1)
            parts = [jnp.where(lo_t, src, 0.0),
                     jnp.where(lo_t, 0.0, rolled),
                     jnp.where(lo_t, rolled, 0.0),
                     jnp.where(lo_t, 0.0, src)]
            dst[BLOCK:, :] = jnp.concatenate(parts, axis=1).astype(BF16)

    def q_job(t):
        q_ref[:, t * nt:(t + 1) * nt] = (cols(t * nt, nt) * (HEAD_DIM ** -0.5)).astype(BF16)

    gates = {}

    def gate_job(which, t):
        gates[which, t] = _sigmoid(cols(o_ab + (2 + which) * d + t * nt, nt))

    mxu_jobs = [kv_job] + [functools.partial(q_job, t) for t in range(q_w // nt)]
    mxu_jobs += [functools.partial(gate_job, w, t) for w in range(2) for t in range(d // nt)]

    conv_out = {}

    def conv_job(r0, c0):
        conv_out[r0, c0] = _conv_piece(s_ref, wdw_ref, r0, c0, rc, lc)

    def ln_job(r0):
        y = jnp.concatenate([conv_out.pop((r0, c0)) for c0 in range(0, d, lc)], axis=1)
        y = y + bdw_ref[...]
        mu = jnp.mean(y, axis=-1, keepdims=True)
        yc = y - mu
        var = jnp.mean(yc * yc, axis=-1, keepdims=True)
        z = yc * lax.rsqrt(var + EPS) * lng_ref[...] + lnb_ref[...]
        y_ref[r0:r0 + rc, :] = (z * _sigmoid(z)).astype(BF16)

    valu_jobs = []
    for r0 in range(0, tm, rc):
        valu_jobs += [functools.partial(conv_job, r0, c0) for c0 in range(0, d, lc)]
        valu_jobs.append(functools.partial(ln_job, r0))
    _interleave(valu_jobs, mxu_jobs)
    s_ref[0:CONV_HALO, :] = s_ref[tm:tm + CONV_HALO, :]

    qi = lax.broadcasted_iota(jnp.int32, (BLOCK, BLOCK), 0)
    cj = lax.broadcasted_iota(jnp.int32, (BLOCK, BLOCK), 1)
    own = cj <= qi
    lo_b = _lane_half_mask(BLOCK)
    ones_lo = jnp.where(_lane_half_mask(2 * BLOCK), 1.0, 0.0).astype(BF16)
    ones_hi = jnp.where(_lane_half_mask(2 * BLOCK), 0.0, 1.0).astype(BF16)

    def attn_job(blk, e):
        band = slice(blk * BLOCK, (blk + 2) * BLOCK)
        rows = slice(blk * BLOCK, (blk + 1) * BLOCK)
        j = (2 * e) // GQA_GROUP
        qp = q_ref[rows, e * pair_w:(e + 1) * pair_w]
        acc = None
        stats = []
        for half in range(2):
            col = (2 * j + half) * pair_w
            kmat = k_ref[band, col:col + pair_w]
            s2 = lax.dot_general(qp, kmat, (((1,), (1,)), ((), ())), preferred_element_type=F32)
            prev = s2[:, :BLOCK] if blk > 0 else jnp.where(i > 0, s2[:, :BLOCK], NEG_BIG)
            sc = jnp.where(own, s2[:, BLOCK:], prev)
            sink = sink_ref[2 * e + half]
            m = jnp.maximum(jnp.max(sc, axis=-1, keepdims=True), sink)
            p = jnp.exp(sc - m)
            p2 = jnp.concatenate([jnp.where(own, 0.0, p), jnp.where(own, p, 0.0)], axis=1)
            vmat = jnp.concatenate([v_ref[band, col:col + pair_w], ones_hi if half else ones_lo],
                                   axis=1)
            pv = jnp.dot(p2.astype(BF16), vmat, preferred_element_type=F32)
            acc = pv if acc is None else acc + pv
            stats.append(jnp.exp(sink - m))
        denom = acc[:, pair_w:] + jnp.where(lo_b, stats[0], stats[1])
        at_ref[rows, e * pair_w:(e + 1) * pair_w] = (acc[:, :pair_w] / denom).astype(BF16)

    y_conv = {}

    def yconv_job(t):
        y_conv[t] = jnp.dot(y_ref[...], wco_ref[:, t * nt:(t + 1) * nt], preferred_element_type=F32)

    attn_jobs = [functools.partial(attn_job, blk, e)
                 for blk in range(tm // BLOCK) for e in range(N_Q_HEADS // 2)]
    _interleave(attn_jobs, [functools.partial(yconv_job, t) for t in range(d // nt)])
    k_ref[0:BLOCK, :] = k_ref[tm:tm + BLOCK, :]
    v_ref[0:BLOCK, :] = v_ref[tm:tm + BLOCK, :]

    y_attn = jnp.dot(at_ref[...], wao_ref[...], preferred_element_type=F32)
    cat = lambda tiles: jnp.concatenate(tiles, axis=1)
    ga = cat([gates[0, t] for t in range(d // nt)])
    gc = cat([gates[1, t] for t in range(d // nt)])
    merged = ga * y_attn + gc * cat([y_conv[t] for t in range(d // nt)])
    out = jnp.dot(merged.astype(BF16), wo_ref[...], preferred_element_type=F32)
    o_ref[0] = x + g2_ref[0] * out


def _mixer(x, shift, scale, g2, norm_g, w_in, sinks, w_dw, b_dw, ln_g, ln_b, w_ao, w_co, w_o,
           tm=256, rc=128, lc=128):
    bsz, s, d = x.shape
    c = w_dw.shape[1]
    row = pl.BlockSpec((1, tm, d), lambda b, i: (b, i, 0))
    per_batch = pl.BlockSpec((1, 1, d), lambda b, i: (b, 0, 0))
    wdw = jnp.concatenate([w_dw, jnp.zeros((CONV_HALO - CONV_WIDTH, c), F32)], axis=0)
    vec = lambda v: v.reshape(1, -1)
    pair_w = 2 * HEAD_DIM
    return pl.pallas_call(
        functools.partial(_mixer_kernel, tm=tm, rc=rc, lc=lc),
        out_shape=jax.ShapeDtypeStruct(x.shape, F32),
        grid=(bsz, s // tm),
        in_specs=[pl.BlockSpec(memory_space=pltpu.SMEM), row, per_batch, per_batch, per_batch,
                  _resident((1, d)), _resident(w_in.shape), _resident(wdw.shape),
                  _resident((1, c)), _resident((1, c)), _resident((1, c)),
                  _resident(w_ao.shape), _resident(w_co.shape), _resident(w_o.shape)],
        out_specs=row,
        scratch_shapes=[
            pltpu.VMEM((tm + CONV_HALO, c), F32),
            pltpu.VMEM((tm + BLOCK, 2 * N_KV_HEADS * pair_w), BF16),
            pltpu.VMEM((tm + BLOCK, 2 * N_KV_HEADS * pair_w), BF16),
            pltpu.VMEM((tm, N_Q_HEADS * HEAD_DIM), BF16),
            pltpu.VMEM((tm, N_Q_HEADS * HEAD_DIM), BF16),
            pltpu.VMEM((tm, c), BF16),
        ],
        compiler_params=_params(2),
        name="mixer",
    )(sinks, x, shift, scale, g2, vec(norm_g), w_in, wdw, vec(b_dw), vec(ln_g), vec(ln_b),
      w_ao, w_co, w_o)


def kernel(x, c, w_ada, b_ada, norm_ffn1_g, ffn1_w_gate, ffn1_w_up, ffn1_w_down, norm_mix_g, w_in,
           attn_sinks, w_attn_o, conv_w_dw, conv_b_dw, conv_ln_g, conv_ln_b, w_conv_o, w_out,
           norm_ffn2_g, ffn2_w_gate, ffn2_w_up, ffn2_w_down, final_norm_g):
    bsz, s, d = x.shape
    depth = w_ada.shape[0]
    bf = lambda w: w.astype(BF16)
    for l in range(depth):
        mod = _ada(c, w_ada[l], b_ada[l]).reshape(bsz, -1, 1, d)
        sh1, sc1, g1, sh2, sc2, g2, sh3, sc3, g3 = (mod[:, k] for k in range(9))
        x = _ffn(x, sh1, sc1, g1, norm_ffn1_g[l], bf(ffn1_w_gate[l]), bf(ffn1_w_up[l]),
                 bf(ffn1_w_down[l]))
        x = _mixer(x, sh2, sc2, g2, norm_mix_g[l], bf(w_in[l]), attn_sinks[l], conv_w_dw[l],
                   conv_b_dw[l], conv_ln_g[l], conv_ln_b[l], bf(w_attn_o[l]), bf(w_conv_o[l]),
                   bf(w_out[l]))
        last = l == depth - 1
        x = _ffn(x, sh3, sc3, g3, norm_ffn2_g[l], bf(ffn2_w_gate[l]), bf(ffn2_w_up[l]),
                 bf(ffn2_w_down[l]), final_g=final_norm_g if last else None)
    return x
```

```python
import functools

import jax
import jax.numpy as jnp
from jax import lax
from jax.experimental import pallas as pl
from jax.experimental.pallas import tpu as pltpu

F32 = jnp.float32
BF16 = jnp.bfloat16

EPS = 1e-6
HEAD_DIM = 64
N_Q_HEADS = 16
N_KV_HEADS = 2
GQA_GROUP = N_Q_HEADS // N_KV_HEADS
BLOCK = 128
CONV_WIDTH = 31
CONV_HALO = 32
SUBLANES = 8
MXU_TILE = 256
NEG_BIG = -1e30

VMEM_LIMIT_BYTES = 56 * 1024 * 1024


def _resident(shape):
    nd = len(shape)
    return pl.BlockSpec(shape, lambda *_: (0,) * nd, pipeline_mode=pl.Buffered(1))


def _params(n_axes):
    return pltpu.CompilerParams(
        dimension_semantics=("arbitrary",) * n_axes,
        vmem_limit_bytes=VMEM_LIMIT_BYTES,
    )


def _sigmoid(x):
    return 1.0 / (1.0 + jnp.exp(-x))


def _norm_mod(x, gain, shift, scale):
    ms = jnp.mean(x * x, axis=-1, keepdims=True)
    y = x * lax.rsqrt(ms + EPS) * gain
    return (y * (1.0 + scale) + shift).astype(BF16)


def _ada_kernel(c_ref, w_ref, b_ref, o_ref):
    c = c_ref[...]
    ca = (c * _sigmoid(c)).astype(BF16)
    o_ref[...] = jnp.dot(ca, w_ref[...].astype(BF16), preferred_element_type=F32) + b_ref[...]


def _ada(c, w, b, tn=1024):
    bsz, d = c.shape
    n = w.shape[1]
    return pl.pallas_call(
        _ada_kernel,
        out_shape=jax.ShapeDtypeStruct((bsz, n), F32),
        grid=(n // tn,),
        in_specs=[
            pl.BlockSpec((bsz, d), lambda j: (0, 0)),
            pl.BlockSpec((d, tn), lambda j: (0, j)),
            pl.BlockSpec((1, tn), lambda j: (0, j)),
        ],
        out_specs=pl.BlockSpec((bsz, tn), lambda j: (0, j)),
        compiler_params=_params(1),
        name="ada",
    )(c, w, b.reshape(1, n))


def _ffn_kernel(x_ref, sh_ref, sc_ref, g_ref, ng_ref, wg_ref, wu_ref, wd_ref, *rest, final_norm):
    if final_norm:
        fg_ref, o_ref = rest
    else:
        (o_ref,) = rest
    x = x_ref[0]
    h = _norm_mod(x, ng_ref[...], sh_ref[0], sc_ref[0])
    gate = jnp.dot(h, wg_ref[...], preferred_element_type=F32)
    up = jnp.dot(h, wu_ref[...], preferred_element_type=F32)
    act = (gate * _sigmoid(gate) * up).astype(BF16)
    dn = jnp.dot(act, wd_ref[...], preferred_element_type=F32)
    xn = x + (0.5 * g_ref[0]) * dn
    if final_norm:
        ms = jnp.mean(xn * xn, axis=-1, keepdims=True)
        xn = xn * lax.rsqrt(ms + EPS) * fg_ref[...]
    o_ref[0] = xn


def _ffn(x, shift, scale, gate, norm_g, wg, wu, wd, final_g=None, tm=512):
    bsz, s, d = x.shape
    dff = wg.shape[1]
    row = pl.BlockSpec((1, tm, d), lambda b, i: (b, i, 0))
    per_batch = pl.BlockSpec((1, 1, d), lambda b, i: (b, 0, 0))
    in_specs = [row, per_batch, per_batch, per_batch, _resident((1, d)),
                _resident((d, dff)), _resident((d, dff)), _resident((dff, d))]
    args = [x, shift, scale, gate, norm_g.reshape(1, d), wg, wu, wd]
    if final_g is not None:
        in_specs.append(_resident((1, d)))
        args.append(final_g.reshape(1, d))
    return pl.pallas_call(
        functools.partial(_ffn_kernel, final_norm=final_g is not None),
        out_shape=jax.ShapeDtypeStruct(x.shape, F32),
        grid=(bsz, s // tm),
        in_specs=in_specs,
        out_specs=row,
        compiler_params=_params(2),
        name="ffn_final" if final_g is not None else "ffn",
    )(*args)


def _lane_half_mask(rows):
    return lax.broadcasted_iota(jnp.int32, (rows, 2 * HEAD_DIM), 1) < HEAD_DIM


def _conv_piece(s_ref, wdw_ref, r0, c0, rc, lc):
    base = CONV_HALO - (CONV_WIDTH - 1)
    wrows = rc + CONV_HALO
    win = s_ref[r0:r0 + wrows, c0:c0 + lc]
    acc = None
    for s in range(SUBLANES):
        sh = win if s == 0 else pltpu.roll(win, wrows - s, axis=0)
        for o in range(base + (s - base) % SUBLANES, base + CONV_WIDTH, SUBLANES):
            k = o - base
            term = wdw_ref[k:k + 1, c0:c0 + lc] * sh[o - s:o - s + rc, :]
            acc = term if acc is None else acc + term
    return acc


def _interleave(primary, filler):
    done = 0
    for idx, job in enumerate(primary):
        job()
        want = ((idx + 1) * len(filler)) // len(primary)
        while done < want:
            filler[done]()
            done += 1


def _mixer_kernel(sink_ref, x_ref, sh_ref, sc_ref, g2_ref, ng_ref, win_ref, wdw_ref, bdw_ref,
                  lng_ref, lnb_ref, wao_ref, wco_ref, wo_ref, o_ref,
                  s_ref, k_ref, v_ref, q_ref, at_ref, y_ref, *, tm, rc, lc):
    i = pl.program_id(1)
    d = x_ref.shape[2]
    q_w = N_Q_HEADS * HEAD_DIM
    pair_w = 2 * HEAD_DIM
    kv_w = N_KV_HEADS * HEAD_DIM
    nt = MXU_TILE

    @pl.when(i == 0)
    def _():
        s_ref[0:CONV_HALO, :] = jnp.zeros((CONV_HALO, s_ref.shape[1]), F32)
        k_ref[0:BLOCK, :] = jnp.zeros((BLOCK, k_ref.shape[1]), BF16)
        v_ref[0:BLOCK, :] = jnp.zeros((BLOCK, v_ref.shape[1]), BF16)

    x = x_ref[0]
    h = _norm_mod(x, ng_ref[...], sh_ref[0], sc_ref[0])

    def cols(lo, n):
        return jnp.dot(h, win_ref[:, lo:lo + n], preferred_element_type=F32)

    o_ab = q_w + 2 * kv_w
    s_ref[CONV_HALO:, :] = cols(o_ab, d) * _sigmoid(cols(o_ab + d, d))

    def kv_job():
        kvp = cols(q_w, 2 * kv_w)
        lo_t = _lane_half_mask(tm)
        for src, dst in ((kvp[:, :kv_w], k_ref), (kvp[:, kv_w:], v_ref)):
            rolled = pltpu.roll(src, HEAD_DIM, axis=1)
            parts = [jnp.where(lo_t, src, 0.0),
                     jnp.where(lo_t, 0.0, rolled),
                     jnp.where(lo_t, rolled, 0.0),
                     jnp.where(lo_t, 0.0, src)]
            dst[BLOCK:, :] = jnp.concatenate(parts, axis=1).astype(BF16)

    def q_job(t):
        q_ref[:, t * nt:(t + 1) * nt] = (cols(t * nt, nt) * (HEAD_DIM ** -0.5)).astype(BF16)

    gates = {}

    def gate_job(which, t):
        gates[which, t] = _sigmoid(cols(o_ab + (2 + which) * d + t * nt, nt))

    mxu_jobs = [kv_job] + [functools.partial(q_job, t) for t in range(q_w // nt)]
    mxu_jobs += [functools.partial(gate_job, w, t) for w in range(2) for t in range(d // nt)]

    conv_out = {}

    def conv_job(r0, c0):
        conv_out[r0, c0] = _conv_piece(s_ref, wdw_ref, r0, c0, rc, lc)

    def ln_job(r0):
        y = jnp.concatenate([conv_out.pop((r0, c0)) for c0 in range(0, d, lc)], axis=1)
        y = y + bdw_ref[...]
        mu = jnp.mean(y, axis=-1, keepdims=True)
        yc = y - mu
        var = jnp.mean(yc * yc, axis=-1, keepdims=True)
        z = yc * lax.rsqrt(var + EPS) * lng_ref[...] + lnb_ref[...]
        y_ref[r0:r0 + rc, :] = (z * _sigmoid(z)).astype(BF16)

    valu_jobs = []
    for r0 in range(0, tm, rc):
        valu_jobs += [functools.partial(conv_job, r0, c0) for c0 in range(0, d, lc)]
        valu_jobs.append(functools.partial(ln_job, r0))
    _interleave(valu_jobs, mxu_jobs)
    s_ref[0:CONV_HALO, :] = s_ref[tm:tm + CONV_HALO, :]

    qi = lax.broadcasted_iota(jnp.int32, (BLOCK, BLOCK), 0)
    cj = lax.broadcasted_iota(jnp.int32, (BLOCK, BLOCK), 1)
    own = cj <= qi
    lo_b = _lane_half_mask(BLOCK)
    ones_lo = jnp.where(_lane_half_mask(2 * BLOCK), 1.0, 0.0).astype(BF16)
    ones_hi = jnp.where(_lane_half_mask(2 * BLOCK), 0.0, 1.0).astype(BF16)

    def attn_job(blk, e):
        band = slice(blk * BLOCK, (blk + 2) * BLOCK)
        rows = slice(blk * BLOCK, (blk + 1) * BLOCK)
        j = (2 * e) // GQA_GROUP
        qp = q_ref[rows, e * pair_w:(e + 1) * pair_w]
        acc = None
        stats = []
        for half in range(2):
            col = (2 * j + half) * pair_w
            kmat = k_ref[band, col:col + pair_w]
            s2 = lax.dot_general(qp, kmat, (((1,), (1,)), ((), ())), preferred_element_type=F32)
            prev = s2[:, :BLOCK] if blk > 0 else jnp.where(i > 0, s2[:, :BLOCK], NEG_BIG)
            sc = jnp.where(own, s2[:, BLOCK:], prev)
            sink = sink_ref[2 * e + half]
            m = jnp.maximum(jnp.max(sc, axis=-1, keepdims=True), sink)
            p = jnp.exp(sc - m)
            p2 = jnp.concatenate([jnp.where(own, 0.0, p), jnp.where(own, p, 0.0)], axis=1)
            vmat = jnp.concatenate([v_ref[band, col:col + pair_w], ones_hi if half else ones_lo],
                                   axis=1)
            pv = jnp.dot(p2.astype(BF16), vmat, preferred_element_type=F32)
            acc = pv if acc is None else acc + pv
            stats.append(jnp.exp(sink - m))
        denom = acc[:, pair_w:] + jnp.where(lo_b, stats[0], stats[1])
        at_ref[rows, e * pair_w:(e + 1) * pair_w] = (acc[:, :pair_w] / denom).astype(BF16)

    y_conv = {}

    def yconv_job(t):
        y_conv[t] = jnp.dot(y_ref[...], wco_ref[:, t * nt:(t + 1) * nt], preferred_element_type=F32)

    attn_jobs = [functools.partial(attn_job, blk, e)
                 for blk in range(tm // BLOCK) for e in range(N_Q_HEADS // 2)]
    _interleave(attn_jobs, [functools.partial(yconv_job, t) for t in range(d // nt)])
    k_ref[0:BLOCK, :] = k_ref[tm:tm + BLOCK, :]
    v_ref[0:BLOCK, :] = v_ref[tm:tm + BLOCK, :]

    y_attn = jnp.dot(at_ref[...], wao_ref[...], preferred_element_type=F32)
    cat = lambda tiles: jnp.concatenate(tiles, axis=1)
    ga = cat([gates[0, t] for t in range(d // nt)])
    gc = cat([gates[1, t] for t in range(d // nt)])
    merged = ga * y_attn + gc * cat([y_conv[t] for t in range(d // nt)])
    out = jnp.dot(merged.astype(BF16), wo_ref[...], preferred_element_type=F32)
    o_ref[0] = x + g2_ref[0] * out


def _mixer(x, shift, scale, g2, norm_g, w_in, sinks, w_dw, b_dw, ln_g, ln_b, w_ao, w_co, w_o,
           tm=512, rc=128, lc=128):
    bsz, s, d = x.shape
    c = w_dw.shape[1]
    row = pl.BlockSpec((1, tm, d), lambda b, i: (b, i, 0))
    per_batch = pl.BlockSpec((1, 1, d), lambda b, i: (b, 0, 0))
    wdw = jnp.concatenate([w_dw, jnp.zeros((CONV_HALO - CONV_WIDTH, c), F32)], axis=0)
    vec = lambda v: v.reshape(1, -1)
    pair_w = 2 * HEAD_DIM
    return pl.pallas_call(
        functools.partial(_mixer_kernel, tm=tm, rc=rc, lc=lc),
        out_shape=jax.ShapeDtypeStruct(x.shape, F32),
        grid=(bsz, s // tm),
        in_specs=[pl.BlockSpec(memory_space=pltpu.SMEM), row, per_batch, per_batch, per_batch,
                  _resident((1, d)), _resident(w_in.shape), _resident(wdw.shape),
                  _resident((1, c)), _resident((1, c)), _resident((1, c)),
                  _resident(w_ao.shape), _resident(w_co.shape), _resident(w_o.shape)],
        out_specs=row,
        scratch_shapes=[
            pltpu.VMEM((tm + CONV_HALO, c), F32),
            pltpu.VMEM((tm + BLOCK, 2 * N_KV_HEADS * pair_w), BF16),
            pltpu.VMEM((tm + BLOCK, 2 * N_KV_HEADS * pair_w), BF16),
            pltpu.VMEM((tm, N_Q_HEADS * HEAD_DIM), BF16),
            pltpu.VMEM((tm, N_Q_HEADS * HEAD_DIM), BF16),
            pltpu.VMEM((tm, c), BF16),
        ],
        compiler_params=_params(2),
        name="mixer",
    )(sinks, x, shift, scale, g2, vec(norm_g), w_in, wdw, vec(b_dw), vec(ln_g), vec(ln_b),
      w_ao, w_co, w_o)


def kernel(x, c, w_ada, b_ada, norm_ffn1_g, ffn1_w_gate, ffn1_w_up, ffn1_w_down, norm_mix_g, w_in,
           attn_sinks, w_attn_o, conv_w_dw, conv_b_dw, conv_ln_g, conv_ln_b, w_conv_o, w_out,
           norm_ffn2_g, ffn2_w_gate, ffn2_w_up, ffn2_w_down, final_norm_g):
    bsz, s, d = x.shape
    depth = w_ada.shape[0]
    bf = lambda w: w.astype(BF16)
    for l in range(depth):
        mod = _ada(c, w_ada[l], b_ada[l]).reshape(bsz, -1, 1, d)
        sh1, sc1, g1, sh2, sc2, g2, sh3, sc3, g3 = (mod[:, k] for k in range(9))
        x = _ffn(x, sh1, sc1, g1, norm_ffn1_g[l], bf(ffn1_w_gate[l]), bf(ffn1_w_up[l]),
                 bf(ffn1_w_down[l]))
        x = _mixer(x, sh2, sc2, g2, norm_mix_g[l], bf(w_in[l]), attn_sinks[l], conv_w_dw[l],
                   conv_b_dw[l], conv_ln_g[l], conv_ln_b[l], bf(w_attn_o[l]), bf(w_conv_o[l]),
                   bf(w_out[l]))
        last = l == depth - 1
        x = _ffn(x, sh3, sc3, g3, norm_ffn2_g[l], bf(ffn2_w_gate[l]), bf(ffn2_w_up[l]),
                 bf(ffn2_w_down[l]), final_g=final_norm_g if last else None)
    return x
```

```python
import functools

import jax
import jax.numpy as jnp
from jax import lax
from jax.experimental import pallas as pl
from jax.experimental.pallas import tpu as pltpu

F32 = jnp.float32
BF16 = jnp.bfloat16

EPS = 1e-6
HEAD_DIM = 64
N_Q_HEADS = 16
N_KV_HEADS = 2
GQA_GROUP = N_Q_HEADS // N_KV_HEADS
BLOCK = 128
CONV_WIDTH = 31
CONV_HALO = 32
SUBLANES = 8
MXU_TILE = 256
NEG_BIG = -1e30

VMEM_LIMIT_BYTES = 56 * 1024 * 1024


def _resident(shape):
    nd = len(shape)
    return pl.BlockSpec(shape, lambda *_: (0,) * nd, pipeline_mode=pl.Buffered(1))


def _params(n_axes):
    return pltpu.CompilerParams(
        dimension_semantics=("arbitrary",) * n_axes,
        vmem_limit_bytes=VMEM_LIMIT_BYTES,
    )


def _sigmoid(x):
    return 1.0 / (1.0 + jnp.exp(-x))


def _norm_mod(x, gain, shift, scale):
    ms = jnp.mean(x * x, axis=-1, keepdims=True)
    y = x * lax.rsqrt(ms + EPS) * gain
    return (y * (1.0 + scale) + shift).astype(BF16)


def _ada_kernel(c_ref, w_ref, b_ref, o_ref):
    c = c_ref[...]
    ca = (c * _sigmoid(c)).astype(BF16)
    o_ref[...] = jnp.dot(ca, w_ref[...].astype(BF16), preferred_element_type=F32) + b_ref[...]


def _ada(c, w, b, tn=1024):
    bsz, d = c.shape
    n = w.shape[1]
    return pl.pallas_call(
        _ada_kernel,
        out_shape=jax.ShapeDtypeStruct((bsz, n), F32),
        grid=(n // tn,),
        in_specs=[
            pl.BlockSpec((bsz, d), lambda j: (0, 0)),
            pl.BlockSpec((d, tn), lambda j: (0, j)),
            pl.BlockSpec((1, tn), lambda j: (0, j)),
        ],
        out_specs=pl.BlockSpec((bsz, tn), lambda j: (0, j)),
        compiler_params=_params(1),
        name="ada",
    )(c, w, b.reshape(1, n))


def _ffn_kernel(x_ref, sh_ref, sc_ref, g_ref, ng_ref, wg_ref, wu_ref, wd_ref, *rest, final_norm):
    if final_norm:
        fg_ref, o_ref = rest
    else:
        (o_ref,) = rest
    x = x_ref[0]
    h = _norm_mod(x, ng_ref[...], sh_ref[0], sc_ref[0])
    gate = jnp.dot(h, wg_ref[...], preferred_element_type=F32)
    up = jnp.dot(h, wu_ref[...], preferred_element_type=F32)
    act = (gate * _sigmoid(gate) * up).astype(BF16)
    dn = jnp.dot(act, wd_ref[...], preferred_element_type=F32)
    xn = x + (0.5 * g_ref[0]) * dn
    if final_norm:
        ms = jnp.mean(xn * xn, axis=-1, keepdims=True)
        xn = xn * lax.rsqrt(ms + EPS) * fg_ref[...]
    o_ref[0] = xn


def _ffn(x, shift, scale, gate, norm_g, wg, wu, wd, final_g=None, tm=512):
    bsz, s, d = x.shape
    dff = wg.shape[1]
    row = pl.BlockSpec((1, tm, d), lambda b, i: (b, i, 0))
    per_batch = pl.BlockSpec((1, 1, d), lambda b, i: (b, 0, 0))
    in_specs = [row, per_batch, per_batch, per_batch, _resident((1, d)),
                _resident((d, dff)), _resident((d, dff)), _resident((dff, d))]
    args = [x, shift, scale, gate, norm_g.reshape(1, d), wg, wu, wd]
    if final_g is not None:
        in_specs.append(_resident((1, d)))
        args.append(final_g.reshape(1, d))
    return pl.pallas_call(
        functools.partial(_ffn_kernel, final_norm=final_g is not None),
        out_shape=jax.ShapeDtypeStruct(x.shape, F32),
        grid=(bsz, s // tm),
        in_specs=in_specs,
        out_specs=row,
        compiler_params=_params(2),
        name="ffn_final" if final_g is not None else "ffn",
    )(*args)


def _lane_half_mask(rows):
    return lax.broadcasted_iota(jnp.int32, (rows, 2 * HEAD_DIM), 1) < HEAD_DIM


def _zero_bits(v):
    u = lax.bitcast_convert_type(v, jnp.uint32)
    acc = u[0:SUBLANES]
    for r in range(SUBLANES, v.shape[0], SUBLANES):
        acc = acc | u[r:r + SUBLANES]
    return lax.shift_right_logical(lax.shift_right_logical(acc, jnp.uint32(16)), jnp.uint32(16))


def _conv_piece(s_ref, wdw_ref, r0, c0, rc, lc, token=None):
    base = CONV_HALO - (CONV_WIDTH - 1)
    wrows = rc + CONV_HALO
    win = s_ref[r0:r0 + wrows, c0:c0 + lc]
    acc = None
    for s in range(SUBLANES):
        sh = win if s == 0 else pltpu.roll(win, wrows - s, axis=0)
        for o in range(base + (s - base) % SUBLANES, base + CONV_WIDTH, SUBLANES):
            k = o - base
            w = wdw_ref[k:k + 1, c0:c0 + lc]
            src = sh[o - s:o - s + rc, :]
            if token is not None and acc is None:
                w8 = lax.bitcast_convert_type(
                    lax.bitcast_convert_type(jnp.broadcast_to(w, (SUBLANES, lc)), jnp.uint32) | token,
                    F32)
                term = jnp.concatenate([w8 * src[0:SUBLANES], w * src[SUBLANES:]], axis=0)
            else:
                term = w * src
            acc = term if acc is None else acc + term
    return acc


def _interleave(primary, filler):
    done = 0
    for idx, job in enumerate(primary):
        job()
        want = ((idx + 1) * len(filler)) // len(primary)
        while done < want:
            filler[done]()
            done += 1


def _mixer_kernel(sink_ref, xn_ref, xc_ref, sh_ref, sc_ref, g2_ref, ng_ref, win_ref, wdw_ref,
                  bdw_ref, lng_ref, lnb_ref, wao_ref, wco_ref, wo_ref, o_ref,
                  hn_ref, hc_ref, sn_ref, s_ref, k_ref, v_ref, q_ref, at_ref, y_ref,
                  *, tm, rc, lc, tiles_per_seq):
    n = pl.program_id(0)
    n_tiles = pl.num_programs(0) - 1
    d = xn_ref.shape[2]
    q_w = N_Q_HEADS * HEAD_DIM
    pair_w = 2 * HEAD_DIM
    kv_w = N_KV_HEADS * HEAD_DIM
    nt = MXU_TILE
    o_ab = q_w + 2 * kv_w
    i_n = lax.rem(jnp.minimum(n, n_tiles - 1), tiles_per_seq)
    i_c = lax.rem(jnp.maximum(n - 1, 0), tiles_per_seq)

    @pl.when(n == 0)
    def _():
        for r in (hn_ref, sn_ref, k_ref, v_ref):
            r[...] = jnp.zeros(r.shape, r.dtype)

    hc_ref[...] = hn_ref[...]
    s_ref[...] = sn_ref[...]
    sn_ref[0:CONV_HALO, :] = jnp.where(i_n > 0, sn_ref[tm:tm + CONV_HALO, :], 0.0)

    @pl.when(i_c == 0)
    def _():
        k_ref[0:BLOCK, :] = jnp.zeros((BLOCK, k_ref.shape[1]), BF16)
        v_ref[0:BLOCK, :] = jnp.zeros((BLOCK, v_ref.shape[1]), BF16)

    h_next = _norm_mod(xn_ref[0], ng_ref[...], sh_ref[0], sc_ref[0])
    hn_ref[...] = h_next
    glu_tok = {}

    def glu_job(t):
        lo = o_ab + t * nt
        a = jnp.dot(h_next, win_ref[:, lo:lo + nt], preferred_element_type=F32)
        b = jnp.dot(h_next, win_ref[:, lo + d:lo + d + nt], preferred_element_type=F32)
        u = a * _sigmoid(b)
        sn_ref[CONV_HALO:, t * nt:(t + 1) * nt] = u
        glu_tok[t] = _zero_bits(u[tm - SUBLANES:, nt - lc:])

    def cols(lo, w):
        return jnp.dot(hc_ref[...], win_ref[:, lo:lo + w], preferred_element_type=F32)

    def kv_job():
        kvp = cols(q_w, 2 * kv_w)
        lo_t = _lane_half_mask(tm)
        for src, dst in ((kvp[:, :kv_w], k_ref), (kvp[:, kv_w:], v_ref)):
            rolled = pltpu.roll(src, HEAD_DIM, axis=1)
            parts = [jnp.where(lo_t, src, 0.0),
                     jnp.where(lo_t, 0.0, rolled),
                     jnp.where(lo_t, rolled, 0.0),
                     jnp.where(lo_t, 0.0, src)]
            dst[BLOCK:, :] = jnp.concatenate(parts, axis=1).astype(BF16)

    def q_job(t):
        q_ref[:, t * nt:(t + 1) * nt] = (cols(t * nt, nt) * (HEAD_DIM ** -0.5)).astype(BF16)

    gates = {}

    def gate_job(which, t):
        gates[which, t] = _sigmoid(cols(o_ab + (2 + which) * d + t * nt, nt))

    mxu_jobs = [kv_job] + [functools.partial(q_job, t) for t in range(q_w // nt)]
    mxu_jobs += [functools.partial(gate_job, w, t) for w in range(2) for t in range(d // nt)]

    conv_out = {}
    pieces = [(r0, c0) for r0 in range(0, tm, rc) for c0 in range(0, d, lc)]
    glu_at = {(t * len(pieces)) // (d // nt) + 1: t for t in range(d // nt)}
    last_tok = [None]

    def conv_job(k):
        r0, c0 = pieces[k]
        tok = last_tok[0]
        if k in glu_at:
            g = glu_tok[glu_at[k]]
            tok = g if tok is None else tok | g
        acc = _conv_piece(s_ref, wdw_ref, r0, c0, rc, lc, tok)
        conv_out[r0, c0] = acc
        last_tok[0] = _zero_bits(acc)

    def ln_job(r0):
        y = jnp.concatenate([conv_out.pop((r0, c0)) for c0 in range(0, d, lc)], axis=1)
        y = y + bdw_ref[...]
        mu = jnp.mean(y, axis=-1, keepdims=True)
        yc = y - mu
        var = jnp.mean(yc * yc, axis=-1, keepdims=True)
        z = yc * lax.rsqrt(var + EPS) * lng_ref[...] + lnb_ref[...]
        y_ref[r0:r0 + rc, :] = (z * _sigmoid(z)).astype(BF16)

    for t in range(d // nt):
        glu_job(t)
    valu_jobs = []
    for k, (r0, c0) in enumerate(pieces):
        valu_jobs.append(functools.partial(conv_job, k))
        if c0 + lc == d:
            valu_jobs.append(functools.partial(ln_job, r0))
    _interleave(valu_jobs, mxu_jobs)

    qi = lax.broadcasted_iota(jnp.int32, (BLOCK, BLOCK), 0)
    cj = lax.broadcasted_iota(jnp.int32, (BLOCK, BLOCK), 1)
    own = cj <= qi
    lo_b = _lane_half_mask(BLOCK)
    ones_lo = jnp.where(_lane_half_mask(2 * BLOCK), 1.0, 0.0).astype(BF16)
    ones_hi = jnp.where(_lane_half_mask(2 * BLOCK), 0.0, 1.0).astype(BF16)

    def attn_job(blk, e):
        band = slice(blk * BLOCK, (blk + 2) * BLOCK)
        rows = slice(blk * BLOCK, (blk + 1) * BLOCK)
        j = (2 * e) // GQA_GROUP
        qp = q_ref[rows, e * pair_w:(e + 1) * pair_w]
        acc = None
        stats = []
        for half in range(2):
            col = (2 * j + half) * pair_w
            kmat = k_ref[band, col:col + pair_w]
            s2 = lax.dot_general(qp, kmat, (((1,), (1,)), ((), ())), preferred_element_type=F32)
            prev = s2[:, :BLOCK] if blk > 0 else jnp.where(i_c > 0, s2[:, :BLOCK], NEG_BIG)
            sc = jnp.where(own, s2[:, BLOCK:], prev)
            sink = sink_ref[2 * e + half]
            m = jnp.maximum(jnp.max(sc, axis=-1, keepdims=True), sink)
            p = jnp.exp(sc - m)
            p2 = jnp.concatenate([jnp.where(own, 0.0, p), jnp.where(own, p, 0.0)], axis=1)
            vmat = jnp.concatenate([v_ref[band, col:col + pair_w], ones_hi if half else ones_lo],
                                   axis=1)
            pv = jnp.dot(p2.astype(BF16), vmat, preferred_element_type=F32)
            acc = pv if acc is None else acc + pv
            stats.append(jnp.exp(sink - m))
        denom = acc[:, pair_w:] + jnp.where(lo_b, stats[0], stats[1])
        at_ref[rows, e * pair_w:(e + 1) * pair_w] = (acc[:, :pair_w] / denom).astype(BF16)

    y_conv = {}

    def yconv_job(t):
        y_conv[t] = jnp.dot(y_ref[...], wco_ref[:, t * nt:(t + 1) * nt], preferred_element_type=F32)

    attn_jobs = [functools.partial(attn_job, blk, e)
                 for blk in range(tm // BLOCK) for e in range(N_Q_HEADS // 2)]
    _interleave(attn_jobs, [functools.partial(yconv_job, t) for t in range(d // nt)])
    k_ref[0:BLOCK, :] = k_ref[tm:tm + BLOCK, :]
    v_ref[0:BLOCK, :] = v_ref[tm:tm + BLOCK, :]

    y_attn = jnp.dot(at_ref[...], wao_ref[...], preferred_element_type=F32)
    cat = lambda tiles: jnp.concatenate(tiles, axis=1)
    ga = cat([gates[0, t] for t in range(d // nt)])
    gc = cat([gates[1, t] for t in range(d // nt)])
    merged = ga * y_attn + gc * cat([y_conv[t] for t in range(d // nt)])
    out = jnp.dot(merged.astype(BF16), wo_ref[...], preferred_element_type=F32)
    o_ref[0] = xc_ref[0] + g2_ref[0] * out


def _mixer(x, shift, scale, g2, norm_g, w_in, sinks, w_dw, b_dw, ln_g, ln_b, w_ao, w_co, w_o,
           tm=512, rc=64, lc=128):
    bsz, s, d = x.shape
    c = w_dw.shape[1]
    tps = s // tm
    n_tiles = bsz * tps

    def tile(n):
        nn = jnp.clip(n, 0, n_tiles - 1)
        return nn // tps, nn % tps

    row_n = pl.BlockSpec((1, tm, d), lambda n: (*tile(n), 0))
    row_c = pl.BlockSpec((1, tm, d), lambda n: (*tile(n - 1), 0))
    mod_n = pl.BlockSpec((1, 1, d), lambda n: (tile(n)[0], 0, 0))
    mod_c = pl.BlockSpec((1, 1, d), lambda n: (tile(n - 1)[0], 0, 0))
    wdw = jnp.concatenate([w_dw, jnp.zeros((CONV_HALO - CONV_WIDTH, c), F32)], axis=0)
    vec = lambda v: v.reshape(1, -1)
    pair_w = 2 * HEAD_DIM
    return pl.pallas_call(
        functools.partial(_mixer_kernel, tm=tm, rc=rc, lc=lc, tiles_per_seq=tps),
        out_shape=jax.ShapeDtypeStruct(x.shape, F32),
        grid=(n_tiles + 1,),
        in_specs=[pl.BlockSpec(memory_space=pltpu.SMEM), row_n, row_c, mod_n, mod_n, mod_c,
                  _resident((1, d)), _resident(w_in.shape), _resident(wdw.shape),
                  _resident((1, c)), _resident((1, c)), _resident((1, c)),
                  _resident(w_ao.shape), _resident(w_co.shape), _resident(w_o.shape)],
        out_specs=row_c,
        scratch_shapes=[
            pltpu.VMEM((tm, d), BF16),
            pltpu.VMEM((tm, d), BF16),
            pltpu.VMEM((tm + CONV_HALO, c), F32),
            pltpu.VMEM((tm + CONV_HALO, c), F32),
            pltpu.VMEM((tm + BLOCK, 2 * N_KV_HEADS * pair_w), BF16),
            pltpu.VMEM((tm + BLOCK, 2 * N_KV_HEADS * pair_w), BF16),
            pltpu.VMEM((tm, N_Q_HEADS * HEAD_DIM), BF16),
            pltpu.VMEM((tm, N_Q_HEADS * HEAD_DIM), BF16),
            pltpu.VMEM((tm, c), BF16),
        ],
        compiler_params=_params(1),
        name="mixer",
    )(sinks, x, x, shift, scale, g2, vec(norm_g), w_in, wdw, vec(b_dw), vec(ln_g), vec(ln_b),
      w_ao, w_co, w_o)


def kernel(x, c, w_ada, b_ada, norm_ffn1_g, ffn1_w_gate, ffn1_w_up, ffn1_w_down, norm_mix_g, w_in,
           attn_sinks, w_attn_o, conv_w_dw, conv_b_dw, conv_ln_g, conv_ln_b, w_conv_o, w_out,
           norm_ffn2_g, ffn2_w_gate, ffn2_w_up, ffn2_w_down, final_norm_g):
    bsz, s, d = x.shape
    depth = w_ada.shape[0]
    bf = lambda w: w.astype(BF16)
    for l in range(depth):
        mod = _ada(c, w_ada[l], b_ada[l]).reshape(bsz, -1, 1, d)
        sh1, sc1, g1, sh2, sc2, g2, sh3, sc3, g3 = (mod[:, k] for k in range(9))
        x = _ffn(x, sh1, sc1, g1, norm_ffn1_g[l], bf(ffn1_w_gate[l]), bf(ffn1_w_up[l]),
                 bf(ffn1_w_down[l]))
        x = _mixer(x, sh2, sc2, g2, norm_mix_g[l], bf(w_in[l]), attn_sinks[l], conv_w_dw[l],
                   conv_b_dw[l], conv_ln_g[l], conv_ln_b[l], bf(w_attn_o[l]), bf(w_conv_o[l]),
                   bf(w_out[l]))
        last = l == depth - 1
        x = _ffn(x, sh3, sc3, g3, norm_ffn2_g[l], bf(ffn2_w_gate[l]), bf(ffn2_w_up[l]),
                 bf(ffn2_w_down[l]), final_g=final_norm_g if last else None)
    return x
```
